```python
import math
import jax
import jax.numpy as jnp
from jax import lax
import numpy as np

D_MODEL = 2048
BATCH = 4
SEQ = 2048
DEPTH = 4
DEC_BATCH = 8
DEC_SEQ = 1
PAST_LEN = 16384
PAGE_SIZE = 128

HEAD_DIM = 128
N_MIX_HEADS = D_MODEL // HEAD_DIM
N_MEM_HEADS = 4
N_SELF_HEADS = N_MIX_HEADS - N_MEM_HEADS
KV_A = 2
G_A = N_SELF_HEADS // KV_A
KV_B = 4
G_B = N_SELF_HEADS // KV_B
IDX_HEADS = 16
IDX_DIM = 64
TOPK_MAX = 256
N_MEM = 256
N_GROUPS = 4
EXPERTS_PER_GROUP = 4
N_EXPERTS = N_GROUPS * EXPERTS_PER_GROUP
TOP_IN_GROUP = 2
D_EXPERT = D_MODEL // 4
Q_BLOCK = 128
N_A_LAYERS = (DEPTH + 1) // 2
N_B_LAYERS = DEPTH // 2
DN_ALPHA = (2 * DEPTH) ** 0.25
DN_BETA = (8 * DEPTH) ** -0.25
LN_EPS = 1e-5
NEG_INF = -1e30
FORGET_BIAS = 4.0

QS = N_SELF_HEADS * HEAD_DIM
QM = N_MEM_HEADS * HEAD_DIM
KA = KV_A * HEAD_DIM
KB = KV_B * HEAD_DIM
QI = IDX_HEADS * IDX_DIM
SPLIT_A = (QS, QS + KA, QS + 2 * KA, QS + 2 * KA + QI, QS + 2 * KA + QI + IDX_DIM,
           QS + 2 * KA + QI + IDX_DIM + IDX_HEADS)
COLS_A = SPLIT_A[-1] + QM
SPLIT_B = (QS, QS + KB, QS + 2 * KB, QS + 2 * KB + N_SELF_HEADS)
COLS_B = SPLIT_B[-1] + QM

kernel_name = "hybrid_dsa_fox_memory_hmoe_step"

F32 = jnp.float32


def alibi_slopes(n):
    def pow2(m):
        start = 2.0 ** (-8.0 / m)
        return [start ** (i + 1) for i in range(m)]
    if math.log2(n).is_integer():
        s = pow2(n)
    else:
        c = 2 ** math.floor(math.log2(n))
        s = pow2(c) + pow2(2 * c)[0::2][: n - c]
    return jnp.asarray(s, F32)


def layer_norm(x, g, b):
    xf = x.astype(F32)
    mu = jnp.mean(xf, -1, keepdims=True)
    var = jnp.mean(jnp.square(xf - mu), -1, keepdims=True)
    return ((xf - mu) * lax.rsqrt(var + LN_EPS) * g + b).astype(x.dtype)


def proj_a(h, w_in):
    B, T, _ = h.shape
    q, k, v, qi, ki, wi, qm = jnp.split(h @ w_in, SPLIT_A, axis=-1)
    return (q.reshape(B, T, N_SELF_HEADS, HEAD_DIM), k.reshape(B, T, KV_A, HEAD_DIM),
            v.reshape(B, T, KV_A, HEAD_DIM), qi.reshape(B, T, IDX_HEADS, IDX_DIM), ki, wi,
            qm.reshape(B, T, N_MEM_HEADS, HEAD_DIM))


def proj_b(h, w_in, b_f):
    B, T, _ = h.shape
    q, k, v, f, qm = jnp.split(h @ w_in, SPLIT_B, axis=-1)
    logf = jax.nn.log_sigmoid(f.astype(F32) + b_f)
    return (q.reshape(B, T, N_SELF_HEADS, HEAD_DIM), k.reshape(B, T, KV_B, HEAD_DIM),
            v.reshape(B, T, KV_B, HEAD_DIM), logf, qm.reshape(B, T, N_MEM_HEADS, HEAD_DIM))


def indexer_scores(qi, wi, ki):
    s = jnp.einsum('qhd,ld->qlh', qi, ki).astype(F32) * (IDX_DIM ** -0.5)
    return jnp.einsum('qlh,qh->ql', jax.nn.relu(s), wi.astype(F32) * (IDX_HEADS ** -0.5))


def sparse_attend(q, k_sel, v_sel, pos_q, pos_sel, slopes):
    tq = q.shape[0]
    qg = q.reshape(tq, KV_A, G_A, HEAD_DIM)
    s = jnp.einsum('qkgd,qnkd->qkgn', qg, k_sel).astype(F32) * (HEAD_DIM ** -0.5)
    dist = (pos_q[:, None] - pos_sel).astype(F32)
    s = s - slopes.reshape(KV_A, G_A)[None, :, :, None] * dist[:, None, None, :]
    s = jnp.where((pos_sel <= pos_q[:, None])[:, None, None, :], s, NEG_INF)
    p = jax.nn.softmax(s, axis=-1).astype(v_sel.dtype)
    o = jnp.einsum('qkgn,qnkd->qkgd', p, v_sel)
    return o.reshape(tq, N_SELF_HEADS, HEAD_DIM)


def dsa_prompt(q, k, v, qi, ki, wi, slopes):
    B, S = q.shape[:2]
    nb = S // Q_BLOCK
    topk = min(TOPK_MAX, S // 4)
    pos_k = jnp.arange(S)

    def blk(args):
        b, j = args
        t0 = j * Q_BLOCK
        sl = lambda a: lax.dynamic_slice_in_dim(a[b], t0, Q_BLOCK, axis=0)
        pos_q = t0 + jnp.arange(Q_BLOCK)
        sc = indexer_scores(sl(qi), sl(wi), ki[b])
        sc = jnp.where(pos_k[None, :] <= pos_q[:, None], sc, NEG_INF)
        _, sel = lax.top_k(sc, topk)
        kb, vb = k[b], v[b]
        return sparse_attend(sl(q), kb[sel], vb[sel], pos_q, sel, slopes)

    bb, jj = jnp.meshgrid(jnp.arange(B), jnp.arange(nb), indexing='ij')
    out = lax.map(blk, (bb.reshape(-1), jj.reshape(-1)))
    return out.reshape(B, S, N_SELF_HEADS, HEAD_DIM)


def dsa_sample(q, k_new, v_new, qi, ki_new, wi, pool_k, pool_v, pool_ki, page_table, slopes):
    Bd, T = q.shape[:2]
    P = page_table.shape[1] * PAGE_SIZE
    L = P + T
    topk = min(TOPK_MAX, L // 4)
    ki_past = pool_ki[page_table].reshape(Bd, P, IDX_DIM)
    ki_all = jnp.concatenate([ki_past, ki_new.astype(ki_past.dtype)], axis=1)
    pos_q = P + jnp.arange(T)
    pos_k = jnp.arange(L)
    sc = jax.vmap(indexer_scores)(qi, wi, ki_all)
    sc = jnp.where(pos_k[None, None, :] <= pos_q[None, :, None], sc, NEG_INF)
    _, sel = lax.top_k(sc, topk)
    is_new = sel >= P
    past_pos = jnp.minimum(sel, P - 1)
    phys = jnp.take_along_axis(page_table, (past_pos // PAGE_SIZE).reshape(Bd, -1),
                               axis=1).reshape(sel.shape)
    off = past_pos % PAGE_SIZE
    new_i = jnp.clip(sel - P, 0, T - 1)
    bidx = jnp.arange(Bd)[:, None, None]

    def pick(pool, new):
        old = pool[phys, off]
        nw = new[bidx, new_i].astype(old.dtype)
        return jnp.where(is_new[..., None, None], nw, old)

    k_sel = pick(pool_k, k_new)
    v_sel = pick(pool_v, v_new)
    return jax.vmap(sparse_attend, in_axes=(0, 0, 0, None, 0, None))(
        q, k_sel, v_sel, pos_q, sel, slopes)


def fox_attend(qg, k, v, c_q, c_k, pos_q, pos_k):
    B, Tq = qg.shape[:2]
    s = jnp.einsum('bqkgd,blkd->bkgql', qg, k).astype(F32) * (HEAD_DIM ** -0.5)
    s = s + c_q.transpose(0, 2, 3, 1)[..., None] - c_k.transpose(0, 2, 3, 1)[..., None, :]
    s = jnp.where(pos_k[None, :] <= pos_q[:, None], s, NEG_INF)
    p = jax.nn.softmax(s, axis=-1).astype(v.dtype)
    o = jnp.einsum('bkgql,blkd->bqkgd', p, v)
    return o.reshape(B, Tq, N_SELF_HEADS, HEAD_DIM)


def fox_prompt(q, k, v, logf):
    B, S = q.shape[:2]
    c = jnp.cumsum(logf.astype(F32), axis=1)
    qg = q.reshape(B, S, KV_B, G_B, HEAD_DIM)
    cg = c.reshape(B, S, KV_B, G_B)
    pos_k = jnp.arange(S)

    def blk(j):
        t0 = j * Q_BLOCK
        qb = lax.dynamic_slice_in_dim(qg, t0, Q_BLOCK, axis=1)
        cb = lax.dynamic_slice_in_dim(cg, t0, Q_BLOCK, axis=1)
        return fox_attend(qb, k, v, cb, cg, t0 + jnp.arange(Q_BLOCK), pos_k)

    out = lax.map(blk, jnp.arange(S // Q_BLOCK))
    return out.transpose(1, 0, 2, 3, 4).reshape(B, S, N_SELF_HEADS, HEAD_DIM)


def fox_sample(q, k_new, v_new, logf_new, pool_k, pool_v, pool_lf, page_table):
    Bd, T = q.shape[:2]
    P = page_table.shape[1] * PAGE_SIZE
    L = P + T
    k_all = jnp.concatenate([pool_k[page_table].reshape(Bd, P, KV_B, HEAD_DIM),
                             k_new.astype(pool_k.dtype)], axis=1)
    v_all = jnp.concatenate([pool_v[page_table].reshape(Bd, P, KV_B, HEAD_DIM),
                             v_new.astype(pool_v.dtype)], axis=1)
    lf_all = jnp.concatenate([pool_lf[page_table].reshape(Bd, P, N_SELF_HEADS).astype(F32),
                              logf_new.astype(F32)], axis=1)
    cg = jnp.cumsum(lf_all, axis=1).reshape(Bd, L, KV_B, G_B)
    qg = q.reshape(Bd, T, KV_B, G_B, HEAD_DIM).astype(k_all.dtype)
    return fox_attend(qg, k_all, v_all, cg[:, P:], cg, P + jnp.arange(T), jnp.arange(L))


def mem_attend(qm, mk, mv):
    s = jnp.einsum('bqhd,bmhd->bhqm', qm, mk.astype(qm.dtype)).astype(F32) * (HEAD_DIM ** -0.5)
    p = jax.nn.softmax(s, axis=-1).astype(qm.dtype)
    return jnp.einsum('bhqm,bmhd->bqhd', p, mv.astype(qm.dtype))


def hier_moe(h, w_r1, b_r1, w_r2, b_r2, w_gate, w_up, w_down):
    B, T, D = h.shape
    xt = h.reshape(B * T, D)
    n = xt.shape[0]
    g_prob = jax.nn.softmax((xt @ w_r1).astype(F32) + b_r1, axis=-1)
    p_grp, grp = lax.top_k(g_prob, 1)
    e_all = jnp.einsum('nd,gde->nge', xt, w_r2).astype(F32) + b_r2
    e_logit = jnp.take_along_axis(e_all, grp[:, :, None], axis=1)[:, 0]
    p_e, e_idx = lax.top_k(jax.nn.softmax(e_logit, axis=-1), TOP_IN_GROUP)
    p_e = p_e / jnp.sum(p_e, -1, keepdims=True)
    expert = grp * EXPERTS_PER_GROUP + e_idx
    gate = jnp.zeros((n, N_EXPERTS), F32).at[jnp.arange(n)[:, None], expert].add(p_grp * p_e)
    hid = jax.nn.silu(jnp.einsum('nd,edf->nef', xt, w_gate)) * jnp.einsum('nd,edf->nef', xt, w_up)
    y = jnp.einsum('nef,efd->nd', hid * gate[:, :, None].astype(hid.dtype), w_down)
    return y.reshape(B, T, D)


def residual_block(h, so, mo, w_o, g1, b1, g2, b2, w_r1, b_r1, w_r2, b_r2, w_gate, w_up, w_down):
    B, T, _ = h.shape
    mix = jnp.concatenate([so.reshape(B, T, QS), mo.reshape(B, T, QM).astype(so.dtype)], -1) @ w_o
    h = layer_norm(DN_ALPHA * h + mix, g1, b1)
    return layer_norm(DN_ALPHA * h + hier_moe(h, w_r1, b_r1, w_r2, b_r2, w_gate, w_up, w_down), g2, b2)


def setup_inputs(seed: int = 0) -> dict:
    key = jax.random.key(seed)
    ks = iter(jax.random.split(key, 40))
    nrm = lambda shape, scale=1.0: jax.random.normal(next(ks), shape, F32) * scale
    n_pages = PAST_LEN // PAGE_SIZE
    n_pool = (5 * DEC_BATCH * n_pages) // 4
    page_table = jax.random.permutation(next(ks), n_pool)[: DEC_BATCH * n_pages].reshape(
        DEC_BATCH, n_pages).astype(jnp.int32)
    col_a = jnp.ones((COLS_A,), F32).at[QS + KA:QS + 2 * KA].set(DN_BETA)
    col_b = jnp.ones((COLS_B,), F32).at[QS + KB:QS + 2 * KB].set(DN_BETA)
    col_m = jnp.ones((2 * QM,), F32).at[QM:].set(DN_BETA)
    sd = D_MODEL ** -0.5
    return {
        "x_prompt": nrm((BATCH, SEQ, D_MODEL)),
        "x_sample": nrm((DEC_BATCH, DEC_SEQ, D_MODEL)),
        "mem_prompt": nrm((BATCH, N_MEM, D_MODEL)),
        "cache_a_k": nrm((N_A_LAYERS, n_pool, PAGE_SIZE, KV_A, HEAD_DIM)),
        "cache_a_v": nrm((N_A_LAYERS, n_pool, PAGE_SIZE, KV_A, HEAD_DIM), DN_BETA),
        "cache_a_ik": nrm((N_A_LAYERS, n_pool, PAGE_SIZE, IDX_DIM)),
        "cache_b_k": nrm((N_B_LAYERS, n_pool, PAGE_SIZE, KV_B, HEAD_DIM)),
        "cache_b_v": nrm((N_B_LAYERS, n_pool, PAGE_SIZE, KV_B, HEAD_DIM), DN_BETA),
        "cache_b_lf": jax.nn.log_sigmoid(nrm((N_B_LAYERS, n_pool, PAGE_SIZE, N_SELF_HEADS)) + FORGET_BIAS),
        "cache_mem_k": nrm((DEPTH, DEC_BATCH, N_MEM, N_MEM_HEADS, HEAD_DIM)),
        "cache_mem_v": nrm((DEPTH, DEC_BATCH, N_MEM, N_MEM_HEADS, HEAD_DIM), DN_BETA),
        "page_table": page_table,
        "w_in_a": nrm((N_A_LAYERS, D_MODEL, COLS_A), sd) * col_a,
        "w_in_b": nrm((N_B_LAYERS, D_MODEL, COLS_B), sd) * col_b,
        "b_f": FORGET_BIAS + nrm((N_B_LAYERS, N_SELF_HEADS), 0.5),
        "w_mem_kv": nrm((DEPTH, D_MODEL, 2 * QM), sd) * col_m,
        "w_o": nrm((DEPTH, QS + QM, D_MODEL), (QS + QM) ** -0.5 * DN_BETA),
        "ln1_g": 1.0 + nrm((DEPTH, D_MODEL), 0.02),
        "ln1_b": nrm((DEPTH, D_MODEL), 0.02),
        "ln2_g": 1.0 + nrm((DEPTH, D_MODEL), 0.02),
        "ln2_b": nrm((DEPTH, D_MODEL), 0.02),
        "w_r1": nrm((DEPTH, D_MODEL, N_GROUPS), sd),
        "b_r1": nrm((DEPTH, N_GROUPS), 0.01),
        "w_r2": nrm((DEPTH, N_GROUPS, D_MODEL, EXPERTS_PER_GROUP), sd),
        "b_r2": nrm((DEPTH, N_GROUPS, EXPERTS_PER_GROUP), 0.01),
        "w_gate": nrm((DEPTH, N_EXPERTS, D_MODEL, D_EXPERT), sd),
        "w_up": nrm((DEPTH, N_EXPERTS, D_MODEL, D_EXPERT), sd),
        "w_down": nrm((DEPTH, N_EXPERTS, D_EXPERT, D_MODEL), D_EXPERT ** -0.5 * DN_BETA),
    }


def reference(x_prompt, x_sample, mem_prompt, cache_a_k, cache_a_v, cache_a_ik, cache_b_k,
              cache_b_v, cache_b_lf, cache_mem_k, cache_mem_v, page_table, w_in_a, w_in_b, b_f,
              w_mem_kv, w_o, ln1_g, ln1_b, ln2_g, ln2_b, w_r1, b_r1, w_r2, b_r2, w_gate, w_up,
              w_down):
    slopes = alibi_slopes(N_SELF_HEADS)
    hp, hs = x_prompt, x_sample
    Bp = hp.shape[0]
    a_k_p, a_v_p, a_ik_p, a_k_s, a_v_s, a_ik_s = [], [], [], [], [], []
    b_k_p, b_v_p, b_lf_p, b_k_s, b_v_s, b_lf_s = [], [], [], [], [], []
    mem_k_p, mem_v_p = [], []
    for i in range(DEPTH):
        j = i // 2
        mkv = (mem_prompt @ w_mem_kv[i]).reshape(Bp, N_MEM, 2, N_MEM_HEADS, HEAD_DIM)
        mk_p, mv_p = mkv[:, :, 0], mkv[:, :, 1]
        mem_k_p.append(mk_p)
        mem_v_p.append(mv_p)
        if i % 2 == 0:
            q, k, v, qi, ki, wi, qm = proj_a(hp, w_in_a[j])
            so_p = dsa_prompt(q, k, v, qi, ki, wi, slopes)
            mo_p = mem_attend(qm, mk_p, mv_p)
            a_k_p.append(k)
            a_v_p.append(v)
            a_ik_p.append(ki)
            q, k, v, qi, ki, wi, qm = proj_a(hs, w_in_a[j])
            so_s = dsa_sample(q, k, v, qi, ki, wi, cache_a_k[j], cache_a_v[j], cache_a_ik[j],
                              page_table, slopes)
            mo_s = mem_attend(qm, cache_mem_k[i], cache_mem_v[i])
            a_k_s.append(k)
            a_v_s.append(v)
            a_ik_s.append(ki)
        else:
            q, k, v, lf, qm = proj_b(hp, w_in_b[j], b_f[j])
            so_p = fox_prompt(q, k, v, lf)
            mo_p = mem_attend(qm, mk_p, mv_p)
            b_k_p.append(k)
            b_v_p.append(v)
            b_lf_p.append(lf)
            q, k, v, lf, qm = proj_b(hs, w_in_b[j], b_f[j])
            so_s = fox_sample(q, k, v, lf, cache_b_k[j], cache_b_v[j], cache_b_lf[j], page_table)
            mo_s = mem_attend(qm, cache_mem_k[i], cache_mem_v[i])
            b_k_s.append(k)
            b_v_s.append(v)
            b_lf_s.append(lf)
        hp = residual_block(hp, so_p, mo_p, w_o[i], ln1_g[i], ln1_b[i], ln2_g[i], ln2_b[i],
                            w_r1[i], b_r1[i], w_r2[i], b_r2[i], w_gate[i], w_up[i], w_down[i])
        hs = residual_block(hs, so_s.astype(hs.dtype), mo_s, w_o[i], ln1_g[i], ln1_b[i], ln2_g[i],
                            ln2_b[i], w_r1[i], b_r1[i], w_r2[i], b_r2[i], w_gate[i], w_up[i],
                            w_down[i])
    return (hp, hs,
            jnp.stack(a_k_p), jnp.stack(a_v_p), jnp.stack(a_ik_p),
            jnp.stack(b_k_p), jnp.stack(b_v_p), jnp.stack(b_lf_p),
            jnp.stack(mem_k_p), jnp.stack(mem_v_p),
            jnp.stack(a_k_s), jnp.stack(a_v_s), jnp.stack(a_ik_s),
            jnp.stack(b_k_s), jnp.stack(b_v_s), jnp.stack(b_lf_s))
```

```python
import functools
import math
from typing import NamedTuple

import jax
import jax.numpy as jnp
from jax import lax
from jax.experimental import pallas as pl
from jax.experimental.pallas import tpu as pltpu

F32 = jnp.float32
BF16 = jnp.bfloat16
I32 = jnp.int32

HEAD_DIM = 128
N_SELF = 12
N_MEMH = 4
KV_A = 2
KV_B = 4
IDX_HEADS = 16
IDX_DIM = 64
TOPK_MAX = 256
N_MEM = 256
N_GROUPS = 4
EPG = 4
N_EXPERTS = N_GROUPS * EPG
PAGE = 128
Q_BLOCK = 128
LN_EPS = 1e-5
NEG_INF = -1e30
QS = N_SELF * HEAD_DIM
QM = N_MEMH * HEAD_DIM
KA = KV_A * HEAD_DIM
KB = KV_B * HEAD_DIM
QI = IDX_HEADS * IDX_DIM
ATT_SCALE = HEAD_DIM ** -0.5
IDX_SCALE = (IDX_DIM ** -0.5) * (IDX_HEADS ** -0.5)

LANES = 128
MM_TN = 512
TOK_TM = 256
MOE_TM = 256
PAD_ROWS = 256
VMEM_LIMIT = 56 * 1024 * 1024
INT_MIN = -2 ** 31


class Cfg(NamedTuple):
    D: int
    B: int
    S: int
    BD: int
    PAST: int
    DEPTH: int
    PP: int

    @property
    def NP(self):
        return self.B * self.S

    @property
    def NT(self):
        return self.NP + PAD_ROWS

    @property
    def NTR(self):
        return self.NP + self.BD

    @property
    def DE(self):
        return self.D // 4

    @property
    def DN_ALPHA(self):
        return (2 * self.DEPTH) ** 0.25

    @property
    def NPAGES(self):
        return self.PAST // PAGE

    @property
    def MAXT(self):
        return (2 * self.NTR + N_EXPERTS * (MOE_TM - 1)) // MOE_TM + 1


def _alibi_slopes(n):
    def pow2(m):
        start = 2.0 ** (-8.0 / m)
        return [start ** (i + 1) for i in range(m)]
    if math.log2(n).is_integer():
        return pow2(n)
    c = 2 ** math.floor(math.log2(n))
    return pow2(c) + pow2(2 * c)[0::2][: n - c]


SLOPES = _alibi_slopes(N_SELF)


def _params(*sem):
    return pltpu.CompilerParams(dimension_semantics=tuple(sem), vmem_limit_bytes=VMEM_LIMIT)


def _dot(a, b):
    return jnp.dot(a, b, preferred_element_type=F32)


def _dot_nt(a, b):
    return lax.dot_general(a, b, (((1,), (1,)), ((), ())), preferred_element_type=F32)


def _split3(x):
    hi = x.astype(BF16)
    r = x - hi.astype(F32)
    mid = r.astype(BF16)
    lo = (r - mid.astype(F32)).astype(BF16)
    return hi, mid, lo


def _dot3(a, b):
    ah, am, _ = _split3(a)
    bh, bm, _ = _split3(b)
    return _dot(ah, bh) + (_dot(ah, bm) + _dot(am, bh))


def _dot3_nt(a, b):
    ah, am, _ = _split3(a)
    bh, bm, _ = _split3(b)
    return _dot_nt(ah, bh) + (_dot_nt(ah, bm) + _dot_nt(am, bh))


def _dot_sel(x, sel_bf):
    h, m, l = _split3(x)
    return _dot(h, sel_bf) + (_dot(m, sel_bf) + _dot(l, sel_bf))


def _layer_norm(x, g, b):
    mu = jnp.mean(x, axis=-1, keepdims=True)
    xc = x - mu
    var = jnp.mean(xc * xc, axis=-1, keepdims=True)
    return xc * lax.rsqrt(var + LN_EPS) * g + b


def _softmax_rows(s):
    m = jnp.max(s, axis=-1, keepdims=True)
    p = jnp.exp(s - m)
    return p, jnp.sum(p, axis=-1, keepdims=True)


def _mm_kernel(x_ref, w_ref, o_ref, wbf_ref):
    @pl.when(pl.program_id(1) == 0)
    def _():
        wbf_ref[...] = w_ref[...].astype(BF16)
    o_ref[...] = _dot(x_ref[...], wbf_ref[...])


def _mm_logsig_kernel(x_ref, w_ref, b_ref, o_ref, wbf_ref):
    @pl.when(pl.program_id(1) == 0)
    def _():
        wbf_ref[...] = w_ref[...].astype(BF16)
    o_ref[...] = jax.nn.log_sigmoid(_dot(x_ref[...], wbf_ref[...]) + b_ref[...])


def _project(x_bf, w, layer, col0, ncol, tn, tm, bias=None):
    M, K = x_bf.shape
    in_specs = [pl.BlockSpec((tm, K), lambda c, r: (r, 0)),
                pl.BlockSpec((None, K, tn), lambda c, r: (layer, 0, c + col0))]
    args = [x_bf, w]
    kern = _mm_kernel
    if bias is not None:
        in_specs.append(pl.BlockSpec((None, 1, tn), lambda c, r: (layer, 0, 0)))
        args.append(bias)
        kern = _mm_logsig_kernel
    return pl.pallas_call(
        kern, grid=(ncol, M // tm), in_specs=in_specs,
        out_specs=pl.BlockSpec((tm, tn), lambda c, r: (r, c)),
        out_shape=jax.ShapeDtypeStruct((M, ncol * tn), F32),
        scratch_shapes=[pltpu.VMEM((K, tn), BF16)],
        compiler_params=_params("arbitrary", "arbitrary"), name="project")(*args)


def _float_key(x):
    bits = lax.bitcast_convert_type(x, I32)
    return bits ^ (lax.shift_right_arithmetic(bits, 31) & jnp.int32(0x7FFFFFFF))


def _topk_mask(key_ref, mask_ref, relevant, k):
    rows, width = key_ref.shape
    kf = jnp.float32(k)

    def count_ge(cand):
        return jnp.sum(jnp.where(key_ref[...] >= cand, 1.0, 0.0), axis=-1, keepdims=True)

    def body(i, t_u):
        cand_u = t_u | lax.shift_left(jnp.int32(1), 31 - i)
        cnt = count_ge(cand_u ^ jnp.int32(INT_MIN))
        return jnp.where(cnt >= kf, cand_u, t_u)

    t_u = lax.fori_loop(0, 32, body, jnp.zeros((rows, 1), I32))
    thr = t_u ^ jnp.int32(INT_MIN)
    key = key_ref[...]
    gt = key > thr
    eq = key == thr
    n_gt = jnp.sum(jnp.where(gt, 1.0, 0.0), axis=-1, keepdims=True)
    n_eq = jnp.sum(jnp.where(eq, 1.0, 0.0), axis=-1, keepdims=True)
    n_eq_rel = jnp.sum(jnp.where(eq & relevant, 1.0, 0.0), axis=-1, keepdims=True)
    need = kf - n_gt
    mask_ref[...] = jnp.where(gt | eq, 1.0, 0.0)
    surplus = jnp.max(jnp.where((n_eq > need) & (n_eq_rel > 0.0), 1.0, 0.0))

    @pl.when(surplus > 0.5)
    def _():
        r_i = lax.broadcasted_iota(I32, (LANES, LANES), 0)
        c_i = lax.broadcasted_iota(I32, (LANES, LANES), 1)
        upper = jnp.where(r_i < c_i, 1.0, 0.0).astype(BF16)

        def chunk(c, carry):
            sl = pl.ds(pl.multiple_of(c * LANES, LANES), LANES)
            kc = key_ref[:, sl]
            eqc = kc == thr
            eqf = jnp.where(eqc, 1.0, 0.0)
            before = _dot(eqf.astype(BF16), upper) + carry
            mask_ref[:, sl] = jnp.where((kc > thr) | (eqc & (before < need)), 1.0, 0.0)
            return carry + jnp.sum(eqf, axis=-1, keepdims=True)

        lax.fori_loop(0, width // LANES, chunk, jnp.zeros((rows, 1), F32))


def _mem_heads(qm, mk, mv, o_ref):
    for hm in range(N_MEMH):
        sl = slice(hm * HEAD_DIM, (hm + 1) * HEAD_DIM)
        s = _dot_nt(qm[:, sl].astype(BF16), mk[:, sl].astype(BF16)) * ATT_SCALE
        p, l = _softmax_rows(s)
        o = _dot(p.astype(BF16), mv[:, sl].astype(BF16)) / l
        o_ref[:, QS + hm * HEAD_DIM:QS + (hm + 1) * HEAD_DIM] = o.astype(o_ref.dtype)


def _dsa_prompt_kernel(q_ref, k_ref, v_ref, qi_ref, kwk_ref, kwq_ref, qm_ref, mk_ref, mv_ref,
                       o_ref, key_ref, mask_ref, *, S, topk):
    j = pl.program_id(1)
    tq = q_ref.shape[0]
    lane = lax.broadcasted_iota(I32, (1, LANES), 1)
    kwk = kwk_ref[...]
    ki2 = jnp.where(lane < IDX_DIM, kwk, pltpu.roll(kwk, IDX_DIM, axis=1)).astype(BF16)
    kwq = kwq_ref[...]
    acc = jnp.zeros((tq, S), F32)
    for pair in range(IDX_HEADS // 2):
        qp = qi_ref[:, pair * LANES:(pair + 1) * LANES]
        for half in range(2):
            keep = (lane < IDX_DIM) if half == 0 else (lane >= IDX_DIM)
            qh = jnp.where(keep, qp, 0.0).astype(BF16)
            s = _dot_nt(qh, ki2)
            col = IDX_DIM + 2 * pair + half
            acc = acc + jnp.maximum(s, 0.0) * (kwq[:, col:col + 1] * IDX_SCALE)
    pos_q = j * tq + lax.broadcasted_iota(I32, (tq, 1), 0)
    pos_k = lax.broadcasted_iota(I32, (1, S), 1)
    causal = pos_k <= pos_q
    key_ref[...] = _float_key(jnp.where(causal, acc, NEG_INF))
    _topk_mask(key_ref, mask_ref, causal, topk)
    keep_mask = (mask_ref[...] > 0.5) & causal
    dist = (pos_q - pos_k).astype(F32)
    for h in range(N_SELF):
        kv = h // (N_SELF // KV_A)
        qh = q_ref[:, h * HEAD_DIM:(h + 1) * HEAD_DIM].astype(BF16)
        kh = k_ref[:, kv * HEAD_DIM:(kv + 1) * HEAD_DIM].astype(BF16)
        vh = v_ref[:, kv * HEAD_DIM:(kv + 1) * HEAD_DIM].astype(BF16)
        s = _dot_nt(qh, kh) * ATT_SCALE - SLOPES[h] * dist
        s = jnp.where(keep_mask, s, NEG_INF)
        p, l = _softmax_rows(s)
        o = _dot(p.astype(BF16), vh) / l
        o_ref[:, h * HEAD_DIM:(h + 1) * HEAD_DIM] = o.astype(o_ref.dtype)
    _mem_heads(qm_ref[...], mk_ref[...], mv_ref[...], o_ref)


def _fox_prompt_kernel(q_ref, k_ref, v_ref, c_ref, ct_ref, qm_ref, mk_ref, mv_ref, o_ref, *, S):
    j = pl.program_id(1)
    tq = q_ref.shape[0]
    pos_q = j * tq + lax.broadcasted_iota(I32, (tq, 1), 0)
    pos_k = lax.broadcasted_iota(I32, (1, S), 1)
    causal = pos_k <= pos_q
    cq = c_ref[...]
    for h in range(N_SELF):
        kv = h // (N_SELF // KV_B)
        qh = q_ref[:, h * HEAD_DIM:(h + 1) * HEAD_DIM].astype(BF16)
        kh = k_ref[:, kv * HEAD_DIM:(kv + 1) * HEAD_DIM].astype(BF16)
        vh = v_ref[:, kv * HEAD_DIM:(kv + 1) * HEAD_DIM].astype(BF16)
        s = _dot_nt(qh, kh) * ATT_SCALE + cq[:, h:h + 1] - ct_ref[h:h + 1, :]
        s = jnp.where(causal, s, NEG_INF)
        p, l = _softmax_rows(s)
        o = _dot(p.astype(BF16), vh) / l
        o_ref[:, h * HEAD_DIM:(h + 1) * HEAD_DIM] = o.astype(o_ref.dtype)
    _mem_heads(qm_ref[...], mk_ref[...], mv_ref[...], o_ref)


def _prompt_attention_a(cfg, layer, ymain, ykw, yqm, memkv):
    B, S, NT = cfg.B, cfg.S, cfg.NT
    tq = Q_BLOCK
    nq = S // tq
    topk = min(TOPK_MAX, S // 4)
    row = lambda b, j: (b * nq + j, 0)
    return pl.pallas_call(
        functools.partial(_dsa_prompt_kernel, S=S, topk=topk),
        grid=(B, nq),
        in_specs=[
            pl.BlockSpec((tq, QS), row),
            pl.BlockSpec((S, KA), lambda b, j: (b, QS // KA)),
            pl.BlockSpec((S, KA), lambda b, j: (b, QS // KA + 1)),
            pl.BlockSpec((tq, QI), lambda b, j: (b * nq + j, (QS + 2 * KA) // QI)),
            pl.BlockSpec((S, LANES), lambda b, j: (b, 0)),
            pl.BlockSpec((tq, LANES), row),
            pl.BlockSpec((tq, QM), row),
            pl.BlockSpec((None, N_MEM, QM), lambda b, j: (layer, b, 0)),
            pl.BlockSpec((None, N_MEM, QM), lambda b, j: (layer, b, 1)),
        ],
        out_specs=pl.BlockSpec((tq, QS + QM), row),
        out_shape=jax.ShapeDtypeStruct((NT, QS + QM), BF16),
        scratch_shapes=[pltpu.VMEM((tq, S), I32), pltpu.VMEM((tq, S), F32)],
        compiler_params=_params("arbitrary", "arbitrary"), name="dsa_prompt",
    )(ymain, ymain, ymain, ymain, ykw, ykw, yqm, memkv, memkv)


def _cumsum_kernel(lf_ref, c_ref, ct_ref, *, S):
    ch = 256
    r_i = lax.broadcasted_iota(I32, (ch, ch), 0)
    c_i = lax.broadcasted_iota(I32, (ch, ch), 1)
    lower = jnp.where(c_i <= r_i, 1.0, 0.0).astype(BF16)
    sel = jnp.where(lax.broadcasted_iota(I32, (16, LANES), 0)
                    == lax.broadcasted_iota(I32, (16, LANES), 1), 1.0, 0.0).astype(BF16)
    carry = jnp.zeros((1, LANES), F32)
    for i in range(S // ch):
        x = lf_ref[i * ch:(i + 1) * ch, :]
        h, m, l = _split3(x)
        c = (_dot(lower, h) + (_dot(lower, m) + _dot(lower, l))) + carry
        c_ref[i * ch:(i + 1) * ch, :] = c
        ch_, cm_, cl_ = _split3(c)
        ct_ref[:, i * ch:(i + 1) * ch] = _dot_nt(sel, ch_) + (_dot_nt(sel, cm_) + _dot_nt(sel, cl_))
        carry = c[ch - 1:ch, :]


def _prompt_cumsum(cfg, lf):
    B, S, NT = cfg.B, cfg.S, cfg.NT
    return pl.pallas_call(
        functools.partial(_cumsum_kernel, S=S), grid=(B,),
        in_specs=[pl.BlockSpec((S, LANES), lambda b: (b, 0))],
        out_specs=[pl.BlockSpec((S, LANES), lambda b: (b, 0)),
                   pl.BlockSpec((None, 16, S), lambda b: (b, 0, 0))],
        out_shape=[jax.ShapeDtypeStruct((NT, LANES), F32), jax.ShapeDtypeStruct((B, 16, S), F32)],
        compiler_params=_params("arbitrary"), name="fox_cumsum")(lf)


def _prompt_attention_b(cfg, layer, ymain, c_tok, c_t, yqm, memkv):
    B, S, NT = cfg.B, cfg.S, cfg.NT
    tq = Q_BLOCK
    nq = S // tq
    row = lambda b, j: (b * nq + j, 0)
    return pl.pallas_call(
        functools.partial(_fox_prompt_kernel, S=S),
        grid=(B, nq),
        in_specs=[
            pl.BlockSpec((tq, QS), row),
            pl.BlockSpec((S, KB), lambda b, j: (b, QS // KB)),
            pl.BlockSpec((S, KB), lambda b, j: (b, QS // KB + 1)),
            pl.BlockSpec((tq, LANES), row),
            pl.BlockSpec((None, 16, S), lambda b, j: (b, 0, 0)),
            pl.BlockSpec((tq, QM), row),
            pl.BlockSpec((None, N_MEM, QM), lambda b, j: (layer, b, 0)),
            pl.BlockSpec((None, N_MEM, QM), lambda b, j: (layer, b, 1)),
        ],
        out_specs=pl.BlockSpec((tq, QS + QM), row),
        out_shape=jax.ShapeDtypeStruct((NT, QS + QM), BF16),
        compiler_params=_params("arbitrary", "arbitrary"), name="fox_prompt",
    )(ymain, ymain, ymain, c_tok, c_t, yqm, memkv, memkv)


def _page_spec(block, layer, i, pp, npages):
    nd = len(block) - 2
    return pl.BlockSpec(
        (None, None) + tuple(block[2:]),
        lambda b, s, pt: (layer, pt[b, jnp.minimum(s * pp + i, npages - 1)]) + (0,) * nd)


def _idx_scores_kernel(pt_ref, qi_ref, wi_ref, kin_ref, *rest, pp, nsteps):
    ki_refs = rest[:pp]
    o_ref = rest[pp]
    s_id = pl.program_id(1)
    qi = qi_ref[...]
    w = wi_ref[...] * IDX_SCALE

    @pl.when(s_id < nsteps - 1)
    def _():
        for i in range(pp):
            s = _dot3_nt(qi, ki_refs[i][...])
            o_ref[:, i * PAGE:(i + 1) * PAGE] = jnp.sum(jnp.maximum(s, 0.0) * w, axis=0, keepdims=True)

    @pl.when(s_id == nsteps - 1)
    def _():
        s = jnp.sum(qi * kin_ref[...], axis=-1, keepdims=True)
        sc = jnp.sum(jnp.maximum(s, 0.0) * w, axis=0, keepdims=True)
        lane = lax.broadcasted_iota(I32, (1, pp * PAGE), 1)
        o_ref[...] = jnp.where(lane == 0, sc, -jnp.inf)


def _sample_idx_scores(cfg, layer, page_table, qi_s, wi_s, ki_new, pool_ki):
    BD, pp = cfg.BD, cfg.PP
    nsteps = cfg.NPAGES // pp + 1
    grid_spec = pltpu.PrefetchScalarGridSpec(
        num_scalar_prefetch=1, grid=(BD, nsteps),
        in_specs=[pl.BlockSpec((None, IDX_HEADS, IDX_DIM), lambda b, s, pt: (b, 0, 0)),
                  pl.BlockSpec((None, IDX_HEADS, 1), lambda b, s, pt: (b, 0, 0)),
                  pl.BlockSpec((None, 1, IDX_DIM), lambda b, s, pt: (b, 0, 0))]
                 + [_page_spec((None, None, PAGE, IDX_DIM), layer, i, pp, cfg.NPAGES) for i in range(pp)],
        out_specs=pl.BlockSpec((None, 1, pp * PAGE), lambda b, s, pt: (b, 0, s)))
    return pl.pallas_call(
        functools.partial(_idx_scores_kernel, pp=pp, nsteps=nsteps), grid_spec=grid_spec,
        out_shape=jax.ShapeDtypeStruct((BD, 1, nsteps * pp * PAGE), F32),
        compiler_params=_params("arbitrary", "arbitrary"), name="sample_idx_scores",
    )(page_table, qi_s, wi_s, ki_new, *([pool_ki] * pp))


def _select_kernel(sc_ref, mask_ref, key_ref, *, topk):
    key_ref[...] = _float_key(sc_ref[...])
    _topk_mask(key_ref, mask_ref, jnp.full(sc_ref.shape, True), topk)


def _sample_select(cfg, scores):
    BD, L = scores.shape
    topk = min(TOPK_MAX, (cfg.PAST + 1) // 4)
    return pl.pallas_call(
        functools.partial(_select_kernel, topk=topk),
        out_shape=jax.ShapeDtypeStruct((BD, L), F32),
        scratch_shapes=[pltpu.VMEM((BD, L), I32)],
        compiler_params=pltpu.CompilerParams(vmem_limit_bytes=VMEM_LIMIT), name="sample_select",
    )(scores)


def _sample_cumsum_kernel(pt_ref, lfn_ref, *rest, pp, nsteps):
    lf_refs = rest[:pp]
    ct_ref, cq_ref, carry_ref = rest[pp:]
    s_id = pl.program_id(1)

    @pl.when(s_id == 0)
    def _():
        carry_ref[...] = jnp.zeros_like(carry_ref)

    @pl.when(s_id < nsteps - 1)
    def _():
        r_i = lax.broadcasted_iota(I32, (PAGE, PAGE), 0)
        c_i = lax.broadcasted_iota(I32, (PAGE, PAGE), 1)
        upper = jnp.where(r_i <= c_i, 1.0, 0.0).astype(BF16)
        carry = carry_ref[...]
        for i in range(pp):
            c = _dot_sel(lf_refs[i][...], upper) + carry[:, 0:1]
            ct_ref[:, i * PAGE:(i + 1) * PAGE] = c
            carry = jnp.broadcast_to(c[:, PAGE - 1:PAGE], carry.shape)
        carry_ref[...] = carry

    @pl.when(s_id == nsteps - 1)
    def _():
        c_new = carry_ref[...] + lfn_ref[...]
        lane = lax.broadcasted_iota(I32, (16, pp * PAGE), 1)
        ct_ref[...] = jnp.where(lane == 0, c_new[:, 0:1], 0.0)
        cq_ref[...] = c_new


def _sample_cumsum(cfg, layer, page_table, lf_new, pool_lft):
    BD, pp = cfg.BD, cfg.PP
    nsteps = cfg.NPAGES // pp + 1
    grid_spec = pltpu.PrefetchScalarGridSpec(
        num_scalar_prefetch=1, grid=(BD, nsteps),
        in_specs=[pl.BlockSpec((None, 16, LANES), lambda b, s, pt: (b, 0, 0))]
                 + [_page_spec((None, None, 16, PAGE), layer, i, pp, cfg.NPAGES) for i in range(pp)],
        out_specs=[pl.BlockSpec((None, 16, pp * PAGE), lambda b, s, pt: (b, 0, s)),
                   pl.BlockSpec((None, 16, LANES), lambda b, s, pt: (b, 0, 0))],
        scratch_shapes=[pltpu.VMEM((16, LANES), F32)])
    return pl.pallas_call(
        functools.partial(_sample_cumsum_kernel, pp=pp, nsteps=nsteps), grid_spec=grid_spec,
        out_shape=[jax.ShapeDtypeStruct((BD, 16, nsteps * pp * PAGE), F32),
                   jax.ShapeDtypeStruct((BD, 16, LANES), F32)],
        compiler_params=_params("arbitrary", "arbitrary"), name="sample_cumsum",
    )(page_table, lf_new, *([pool_lft] * pp))


def _sample_attn_kernel(pt_ref, q_ref, kn_ref, vn_ref, bias_ref, cq_ref, slope_ref, qm_ref,
                        mk_ref, mv_ref, *rest, pp, nsteps, kvh, past, fox):
    k_refs = rest[:pp]
    v_refs = rest[pp:2 * pp]
    o_ref, om_ref, m_ref, l_ref, acc_ref = rest[2 * pp:]
    s_id = pl.program_id(1)
    R = PAGE * kvh
    gsz = N_SELF // kvh
    q = q_ref[...]
    row = lax.broadcasted_iota(I32, (16, 1), 0)
    kv_of_row = jnp.zeros((16, 1), I32)
    for t in range(1, kvh):
        kv_of_row = kv_of_row + jnp.where(row >= t * gsz, 1, 0)

    @pl.when(s_id == 0)
    def _():
        m_ref[...] = jnp.full_like(m_ref, NEG_INF)
        l_ref[...] = jnp.zeros_like(l_ref)
        acc_ref[...] = jnp.zeros_like(acc_ref)

    def update(s, keep, pv):
        m_old = m_ref[...]
        m_new = jnp.maximum(m_old, jnp.max(jnp.where(keep, s, NEG_INF), axis=-1, keepdims=True))
        alpha = jnp.exp(m_old - m_new)
        p = jnp.where(keep, jnp.exp(s - m_new), 0.0)
        l_ref[...] = alpha * l_ref[...] + jnp.sum(p, axis=-1, keepdims=True)
        acc_ref[...] = alpha * acc_ref[...] + pv(p)
        m_ref[...] = m_new

    @pl.when(s_id < nsteps - 1)
    def _():
        qb = q.astype(BF16)
        col = lax.broadcasted_iota(I32, (1, R), 1)
        kv_of_col = col & (kvh - 1)
        key_in_page = lax.shift_right_logical(col, kvh.bit_length() - 1)
        e_r = lax.broadcasted_iota(I32, (PAGE, R), 0)
        e_c = lax.shift_right_logical(lax.broadcasted_iota(I32, (PAGE, R), 1), kvh.bit_length() - 1)
        expand = jnp.where(e_r == e_c, 1.0, 0.0).astype(BF16)
        s_parts, keep_parts = [], []
        for i in range(pp):
            s = _dot_nt(qb, k_refs[i][...].astype(BF16)) * ATT_SCALE
            b = bias_ref[:, i * PAGE:(i + 1) * PAGE]
            if fox:
                s = (s + cq_ref[:, 0:1]) - _dot_sel(b, expand)
                keep = kv_of_col == kv_of_row
            else:
                pos_k = (s_id * pp + i) * PAGE + key_in_page
                s = s - slope_ref[:, 0:1] * (past - pos_k).astype(F32)
                sel = _dot(jnp.broadcast_to(b, (8, PAGE)).astype(BF16), expand)[0:1, :]
                keep = (kv_of_col == kv_of_row) & (sel > 0.5)
            s_parts.append(s)
            keep_parts.append(keep)
        s_all = jnp.concatenate(s_parts, axis=-1)
        keep_all = jnp.concatenate(keep_parts, axis=-1)

        def pv(p):
            pb = p.astype(BF16)
            out = _dot(pb[:, 0:R], v_refs[0][...].astype(BF16))
            for i in range(1, pp):
                out = out + _dot(pb[:, i * R:(i + 1) * R], v_refs[i][...].astype(BF16))
            return out

        update(s_all, keep_all, pv)

    @pl.when(s_id == nsteps - 1)
    def _():
        s = jnp.sum(q * kn_ref[...], axis=-1, keepdims=True) * ATT_SCALE
        b = bias_ref[:, 0:1]
        if fox:
            s = (s + cq_ref[:, 0:1]) - b
            keep = jnp.full((16, 1), True)
        else:
            keep = jnp.broadcast_to(b > 0.5, (16, 1))
        update(s, keep, lambda p: p * vn_ref[...])
        o_ref[...] = acc_ref[...] / l_ref[...]
        ncol = mk_ref.shape[0]
        colm = lax.broadcasted_iota(I32, (1, ncol), 1) & (N_MEMH - 1)
        rowm = lax.broadcasted_iota(I32, (8, 1), 0) & (N_MEMH - 1)
        sm = _dot_nt(qm_ref[...].astype(BF16), mk_ref[...].astype(BF16)) * ATT_SCALE
        sm = jnp.where(colm == rowm, sm, NEG_INF)
        pm, lm = _softmax_rows(sm)
        om_ref[...] = _dot(pm.astype(BF16), mv_ref[...].astype(BF16)) / lm


def _sample_attention(cfg, layer, mem_layer, page_table, q16, kn16, vn16, bias, cq, slopes16, qm8,
                      pool_k, pool_v, cache_mk, cache_mv, *, kvh, fox):
    BD, pp = cfg.BD, cfg.PP
    nsteps = cfg.NPAGES // pp + 1
    R = PAGE * kvh
    brow = 16 if fox else 1
    per_b = lambda b, s, pt: (b, 0, 0)
    grid_spec = pltpu.PrefetchScalarGridSpec(
        num_scalar_prefetch=1, grid=(BD, nsteps),
        in_specs=[pl.BlockSpec((None, 16, HEAD_DIM), per_b),
                  pl.BlockSpec((None, 16, HEAD_DIM), per_b),
                  pl.BlockSpec((None, 16, HEAD_DIM), per_b),
                  pl.BlockSpec((None, brow, pp * PAGE), lambda b, s, pt: (b, 0, s)),
                  pl.BlockSpec((None, 16, LANES), per_b),
                  pl.BlockSpec((16, LANES), lambda b, s, pt: (0, 0)),
                  pl.BlockSpec((None, 8, HEAD_DIM), per_b),
                  pl.BlockSpec((None, None, N_MEM * N_MEMH, HEAD_DIM), lambda b, s, pt: (mem_layer, b, 0, 0)),
                  pl.BlockSpec((None, None, N_MEM * N_MEMH, HEAD_DIM), lambda b, s, pt: (mem_layer, b, 0, 0))]
                 + [_page_spec((None, None, R, HEAD_DIM), layer, i, pp, cfg.NPAGES) for i in range(pp)]
                 + [_page_spec((None, None, R, HEAD_DIM), layer, i, pp, cfg.NPAGES) for i in range(pp)],
        out_specs=[pl.BlockSpec((None, 16, HEAD_DIM), per_b),
                   pl.BlockSpec((None, 8, HEAD_DIM), per_b)],
        scratch_shapes=[pltpu.VMEM((16, 1), F32), pltpu.VMEM((16, 1), F32), pltpu.VMEM((16, HEAD_DIM), F32)])
    return pl.pallas_call(
        functools.partial(_sample_attn_kernel, pp=pp, nsteps=nsteps, kvh=kvh, past=cfg.PAST, fox=fox),
        grid_spec=grid_spec,
        out_shape=[jax.ShapeDtypeStruct((BD, 16, HEAD_DIM), F32),
                   jax.ShapeDtypeStruct((BD, 8, HEAD_DIM), F32)],
        compiler_params=_params("arbitrary", "arbitrary"),
        name="sample_attention_fox" if fox else "sample_attention_dsa",
    )(page_table, q16, kn16, vn16, bias, cq, slopes16, qm8, cache_mk, cache_mv,
      *([pool_k] * pp), *([pool_v] * pp))


def _outproj_kernel(a_ref, h_ref, w_ref, g_ref, b_ref, wr_ref, br_ref, h1_ref, route_ref, cnt_ref,
                    carry_ref, *, alpha, ntr):
    i = pl.program_id(0)
    tm = a_ref.shape[0]

    @pl.when(i == 0)
    def _():
        carry_ref[...] = jnp.zeros_like(carry_ref)

    h1 = _layer_norm(alpha * h_ref[...] + _dot(a_ref[...], w_ref[...]), g_ref[...], b_ref[...])
    h1_ref[...] = h1
    logits = _dot3(h1, wr_ref[...]) + br_ref[...]
    lane = lax.broadcasted_iota(I32, (1, LANES), 1)
    lanef = lane.astype(F32)
    ninf = -jnp.inf

    def first_max(x):
        m = jnp.max(x, axis=-1, keepdims=True)
        idx = jnp.min(jnp.where(x == m, lanef, float(LANES)), axis=-1, keepdims=True)
        return m, idx

    gl = jnp.where(lane < N_GROUPS, logits, ninf)
    mg, grp = first_max(gl)
    p_grp = 1.0 / jnp.sum(jnp.exp(gl - mg), axis=-1, keepdims=True)
    lo = N_GROUPS + EPG * grp
    el = jnp.where((lanef >= lo) & (lanef < lo + EPG), logits, ninf)
    m1, e1 = first_max(el)
    m2, e2 = first_max(jnp.where(lanef == e1, ninf, el))
    z = jnp.sum(jnp.exp(el - m1), axis=-1, keepdims=True)
    p1 = 1.0 / z
    p2 = jnp.exp(m2 - m1) / z
    g1 = p_grp * (p1 / (p1 + p2))
    g2 = p_grp * (p2 / (p1 + p2))

    valid = (i * tm + lax.broadcasted_iota(I32, (tm, 1), 0)) < ntr
    hot = jnp.where(valid & ((lanef == e1) | (lanef == e2)), 1.0, 0.0)
    r_i = lax.broadcasted_iota(I32, (tm, tm), 0)
    c_i = lax.broadcasted_iota(I32, (tm, tm), 1)
    before = _dot(jnp.where(c_i < r_i, 1.0, 0.0).astype(BF16), hot.astype(BF16)) + carry_ref[0:1, :]
    r1 = jnp.sum(jnp.where(lanef == e1, before, 0.0), axis=-1, keepdims=True)
    r2 = jnp.sum(jnp.where(lanef == e2, before, 0.0), axis=-1, keepdims=True)
    carry_ref[...] = carry_ref[...] + jnp.sum(hot, axis=0, keepdims=True)
    cnt_ref[...] = carry_ref[...]
    route = jnp.where(lane == 0, e1 - N_GROUPS,
            jnp.where(lane == 1, e2 - N_GROUPS,
            jnp.where(lane == 2, r1,
            jnp.where(lane == 3, r2,
            jnp.where(lane == 4, g1,
            jnp.where(lane == 5, g2, 0.0))))))
    route_ref[...] = jnp.where(valid, route, 0.0)


def _outproj_ln_route(cfg, layer, attn, h, w_o_bf, ln_g, ln_b, wr, br):
    NT, D = h.shape
    tm = TOK_TM
    KO = attn.shape[1]
    par = lambda i: (layer, 0, 0)
    return pl.pallas_call(
        functools.partial(_outproj_kernel, alpha=cfg.DN_ALPHA, ntr=cfg.NTR), grid=(NT // tm,),
        in_specs=[pl.BlockSpec((tm, KO), lambda i: (i, 0)),
                  pl.BlockSpec((tm, D), lambda i: (i, 0)),
                  pl.BlockSpec((None, KO, D), par),
                  pl.BlockSpec((None, 1, D), par),
                  pl.BlockSpec((None, 1, D), par),
                  pl.BlockSpec((None, D, LANES), par),
                  pl.BlockSpec((None, 1, LANES), par)],
        out_specs=[pl.BlockSpec((tm, D), lambda i: (i, 0)),
                   pl.BlockSpec((tm, LANES), lambda i: (i, 0)),
                   pl.BlockSpec((8, LANES), lambda i: (0, 0))],
        out_shape=[jax.ShapeDtypeStruct((NT, D), F32), jax.ShapeDtypeStruct((NT, LANES), F32),
                   jax.ShapeDtypeStruct((8, LANES), F32)],
        scratch_shapes=[pltpu.VMEM((8, LANES), F32)],
        compiler_params=_params("arbitrary"), name="outproj_ln_route",
    )(attn, h, w_o_bf, ln_g, ln_b, wr, br)


def _scatter_kernel(cnt_ref, e1_ref, e2_ref, r1_ref, r2_ref, h_ref, xs_ref, pos1_ref, pos2_ref,
                    te_ref, na_ref, off_ref, sem, *, ntr, maxt):
    off_ref[0] = 0
    for e in range(N_EXPERTS):
        tiles = lax.shift_right_logical(cnt_ref[e] + (MOE_TM - 1), MOE_TM.bit_length() - 1)
        off_ref[e + 1] = off_ref[e] + tiles * MOE_TM
    n_active = lax.shift_right_logical(off_ref[N_EXPERTS], MOE_TM.bit_length() - 1)
    na_ref[0] = n_active

    def fill_tiles(t, _):
        start = t * MOE_TM
        e_of = jnp.int32(0)
        for e in range(1, N_EXPERTS):
            e_of = e_of + jnp.where(start >= off_ref[e], 1, 0).astype(I32)
        te_ref[t] = e_of
        return 0
    lax.fori_loop(0, maxt, fill_tiles, 0)

    def row_copy(src, dst):
        return pltpu.make_async_copy(h_ref.at[pl.ds(src, 1)], xs_ref.at[pl.ds(dst, 1)], sem)

    chunk = 128
    nchunks = (ntr + chunk - 1) // chunk

    def issue(c, _):
        def one(t, _):
            p1 = off_ref[e1_ref[t]] + r1_ref[t]
            p2 = off_ref[e2_ref[t]] + r2_ref[t]
            pos1_ref[t] = p1
            pos2_ref[t] = p2
            row_copy(t, p1).start()
            row_copy(t, p2).start()
            return 0
        lo = c * chunk
        hi = jnp.minimum(lo + chunk, ntr)
        lax.fori_loop(lo, hi, one, 0)
        return 0

    def drain(c, _):
        lo = c * chunk
        hi = jnp.minimum(lo + chunk, ntr)

        def one(t, _):
            row_copy(0, 0).wait()
            row_copy(0, 0).wait()
            return 0
        lax.fori_loop(lo, hi, one, 0)
        return 0

    def step(c, _):
        issue(c, 0)

        @pl.when(c > 0)
        def _():
            drain(c - 1, 0)
        return 0
    lax.fori_loop(0, nchunks, step, 0)
    drain(nchunks - 1, 0)

    def pad_expert(e, _):
        lo = off_ref[e] + cnt_ref[e]
        hi = off_ref[e + 1]

        def one(p, _):
            row_copy(ntr, p).start()
            return 0
        lax.fori_loop(lo, hi, one, 0)

        def one_w(p, _):
            row_copy(0, 0).wait()
            return 0
        lax.fori_loop(lo, hi, one_w, 0)
        return 0
    lax.fori_loop(0, N_EXPERTS, pad_expert, 0)

    def tail(t, _):
        pos1_ref[t] = 0
        pos2_ref[t] = 0
        return 0
    lax.fori_loop(ntr, pos1_ref.shape[0], tail, 0)


def _plan_scatter(cfg, cnt, e1, e2, r1, r2, h1):
    NT, D = h1.shape
    maxt = cfg.MAXT
    smem = pl.BlockSpec(memory_space=pltpu.SMEM)
    return pl.pallas_call(
        functools.partial(_scatter_kernel, ntr=cfg.NTR, maxt=maxt),
        in_specs=[smem, smem, smem, smem, smem, pl.BlockSpec(memory_space=pl.ANY)],
        out_specs=[pl.BlockSpec(memory_space=pl.ANY), smem, smem, smem, smem],
        out_shape=[jax.ShapeDtypeStruct((maxt * MOE_TM, D), F32),
                   jax.ShapeDtypeStruct(e1.shape, I32), jax.ShapeDtypeStruct(e1.shape, I32),
                   jax.ShapeDtypeStruct((maxt,), I32), jax.ShapeDtypeStruct((1,), I32)],
        scratch_shapes=[pltpu.SMEM((N_EXPERTS + 1,), I32), pltpu.SemaphoreType.DMA],
        compiler_params=pltpu.CompilerParams(vmem_limit_bytes=VMEM_LIMIT, has_side_effects=True),
        name="plan_scatter",
    )(cnt, e1, e2, r1, r2, h1)


def _expert_kernel(te_ref, na_ref, x_ref, wg_ref, wu_ref, wd_ref, y_ref, wgb_ref, wub_ref, wdb_ref):
    i = pl.program_id(0)
    prev = te_ref[jnp.maximum(i - 1, 0)]
    active = i < na_ref[0]

    @pl.when(active & ((i == 0) | (te_ref[i] != prev)))
    def _():
        wgb_ref[...] = wg_ref[...].astype(BF16)
        wub_ref[...] = wu_ref[...].astype(BF16)
        wdb_ref[...] = wd_ref[...].astype(BF16)

    @pl.when(active)
    def _():
        xb = x_ref[...].astype(BF16)
        g = _dot(xb, wgb_ref[...])
        u = _dot(xb, wub_ref[...])
        hid = (g * jax.nn.sigmoid(g)) * u
        y_ref[...] = _dot(hid.astype(BF16), wdb_ref[...])


def _expert_mlp(cfg, layer, te, na, xs, w_gate, w_up, w_down):
    D, DE = cfg.D, cfg.DE
    maxt = cfg.MAXT
    tile = lambda i, te, na: (jnp.minimum(i, na[0] - 1), 0)
    wsel = lambda i, te, na: (layer, te[jnp.minimum(i, na[0] - 1)], 0, 0)
    grid_spec = pltpu.PrefetchScalarGridSpec(
        num_scalar_prefetch=2, grid=(maxt,),
        in_specs=[pl.BlockSpec((MOE_TM, D), tile),
                  pl.BlockSpec((None, None, D, DE), wsel),
                  pl.BlockSpec((None, None, D, DE), wsel),
                  pl.BlockSpec((None, None, DE, D), wsel)],
        out_specs=pl.BlockSpec((MOE_TM, D), tile),
        scratch_shapes=[pltpu.VMEM((D, DE), BF16), pltpu.VMEM((D, DE), BF16), pltpu.VMEM((DE, D), BF16)])
    return pl.pallas_call(
        _expert_kernel, grid_spec=grid_spec,
        out_shape=jax.ShapeDtypeStruct((maxt * MOE_TM, D), F32),
        compiler_params=_params("arbitrary"), name="expert_mlp",
    )(te, na, xs, w_gate, w_up, w_down)


def _combine_kernel(pos1_ref, pos2_ref, ys_ref, h1_ref, route_ref, g_ref, b_ref, h2_ref, h2b_ref,
                    y1_ref, y2_ref, sem, *, alpha):
    i = pl.program_id(0)
    tm = h1_ref.shape[0]

    def copies(r):
        t = i * tm + r
        return (pltpu.make_async_copy(ys_ref.at[pl.ds(pos1_ref[t], 1)], y1_ref.at[pl.ds(r, 1)], sem),
                pltpu.make_async_copy(ys_ref.at[pl.ds(pos2_ref[t], 1)], y2_ref.at[pl.ds(r, 1)], sem))

    def start(r, _):
        a, b = copies(r)
        a.start()
        b.start()
        return 0
    lax.fori_loop(0, tm, start, 0)

    def wait(r, _):
        a, b = copies(r)
        a.wait()
        b.wait()
        return 0
    lax.fori_loop(0, tm, wait, 0)

    route = route_ref[...]
    moe = route[:, 4:5] * y1_ref[...] + route[:, 5:6] * y2_ref[...]
    h2 = _layer_norm(alpha * h1_ref[...] + moe, g_ref[...], b_ref[...])
    h2_ref[...] = h2
    h2b_ref[...] = h2.astype(BF16)


def _combine_ln(cfg, layer, pos1, pos2, ys, h1, route, ln_g, ln_b):
    NT, D = h1.shape
    tm = TOK_TM
    par = lambda i, p1, p2: (layer, 0, 0)
    rows = lambda i, p1, p2: (i, 0)
    grid_spec = pltpu.PrefetchScalarGridSpec(
        num_scalar_prefetch=2, grid=(NT // tm,),
        in_specs=[pl.BlockSpec(memory_space=pl.ANY),
                  pl.BlockSpec((tm, D), rows),
                  pl.BlockSpec((tm, LANES), rows),
                  pl.BlockSpec((None, 1, D), par),
                  pl.BlockSpec((None, 1, D), par)],
        out_specs=[pl.BlockSpec((tm, D), rows), pl.BlockSpec((tm, D), rows)],
        scratch_shapes=[pltpu.VMEM((tm, D), F32), pltpu.VMEM((tm, D), F32), pltpu.SemaphoreType.DMA])
    return pl.pallas_call(
        functools.partial(_combine_kernel, alpha=cfg.DN_ALPHA), grid_spec=grid_spec,
        out_shape=[jax.ShapeDtypeStruct((NT, D), F32), jax.ShapeDtypeStruct((NT, D), BF16)],
        compiler_params=_params("arbitrary"), name="combine_ln",
    )(pos1, pos2, ys, h1, route, ln_g, ln_b)


def _pad_last(x, n):
    return jnp.pad(x, [(0, 0)] * (x.ndim - 1) + [(0, n - x.shape[-1])])


def _moe_layer(cfg, i, attn, h, w_o_bf, ln1_g, ln1_b, ln2_g, ln2_b, wr, br, w_gate, w_up, w_down):
    NT = cfg.NT
    h1, route, cnt = _outproj_ln_route(cfg, i, attn, h, w_o_bf, ln1_g, ln1_b, wr, br)
    npad = ((NT + 1023) // 1024) * 1024
    ri = route[:, 0:4].astype(I32)
    col = lambda c: jnp.pad(ri[:, c], (0, npad - NT))
    cnt_i = cnt[0, N_GROUPS:N_GROUPS + N_EXPERTS].astype(I32)
    xs, pos1, pos2, te, na = _plan_scatter(cfg, cnt_i, col(0), col(1), col(2), col(3), h1)
    ys = _expert_mlp(cfg, i, te, na, xs, w_gate, w_up, w_down)
    return _combine_ln(cfg, i, pos1, pos2, ys, h1, route, ln2_g, ln2_b)


def _forward(cfg, x_prompt, x_sample, mem_prompt, cache_a_k, cache_a_v, cache_a_ik, cache_b_k,
             cache_b_v, cache_b_lf, cache_mem_k, cache_mem_v, page_table, w_in_a, w_in_b, b_f,
             w_mem_kv, w_o, ln1_g, ln1_b, ln2_g, ln2_b, w_r1, b_r1, w_r2, b_r2, w_gate, w_up, w_down):
    D, B, S, BD, NP, NT, DEPTH = cfg.D, cfg.B, cfg.S, cfg.BD, cfg.NP, cfg.NT, cfg.DEPTH
    n_pool = cache_a_k.shape[1]
    tm_mm = NT // 8

    ca = QS + 2 * KA
    w_a_kw = _pad_last(w_in_a[:, :, ca + QI:ca + QI + IDX_DIM + IDX_HEADS], LANES)
    w_a_qm = w_in_a[:, :, ca + QI + IDX_DIM + IDX_HEADS:]
    cb = QS + 2 * KB
    w_b_f = _pad_last(w_in_b[:, :, cb:cb + N_SELF], LANES)
    w_b_qm = w_in_b[:, :, cb + N_SELF:]
    b_f_pad = _pad_last(b_f, LANES)[:, None, :]
    w_o_bf = w_o.astype(BF16)
    wr = _pad_last(jnp.concatenate(
        [w_r1, jnp.transpose(w_r2, (0, 2, 1, 3)).reshape(DEPTH, D, N_EXPERTS)], axis=-1), LANES)
    br = _pad_last(jnp.concatenate([b_r1, b_r2.reshape(DEPTH, N_EXPERTS)], axis=-1), LANES)[:, None, :]
    g1, b1, g2, b2 = (a[:, None, :] for a in (ln1_g, ln1_b, ln2_g, ln2_b))
    pool_ak = cache_a_k.reshape(cache_a_k.shape[0], n_pool, PAGE * KV_A, HEAD_DIM)
    pool_av = cache_a_v.reshape(cache_a_v.shape[0], n_pool, PAGE * KV_A, HEAD_DIM)
    pool_bk = cache_b_k.reshape(cache_b_k.shape[0], n_pool, PAGE * KV_B, HEAD_DIM)
    pool_bv = cache_b_v.reshape(cache_b_v.shape[0], n_pool, PAGE * KV_B, HEAD_DIM)
    pool_lft = jnp.pad(jnp.swapaxes(cache_b_lf, -1, -2), ((0, 0), (0, 0), (0, 16 - N_SELF), (0, 0)))
    cmk = cache_mem_k.reshape(DEPTH, BD, N_MEM * N_MEMH, HEAD_DIM)
    cmv = cache_mem_v.reshape(DEPTH, BD, N_MEM * N_MEMH, HEAD_DIM)
    slopes16 = jnp.broadcast_to(jnp.asarray(SLOPES + [0.0] * 4, F32)[:, None], (16, LANES))

    h = jnp.concatenate([x_prompt.reshape(NP, D), x_sample.reshape(BD, D),
                         jnp.zeros((PAD_ROWS - BD, D), F32)], axis=0)
    h_bf = h.astype(BF16)

    mem_bf = mem_prompt.reshape(B * N_MEM, D).astype(BF16)
    memkv = jnp.stack([_project(mem_bf, w_mem_kv, i, 0, 2, QM, B * N_MEM) for i in range(DEPTH)])

    def heads16(x, n):
        return jnp.pad(x.reshape(BD, n, HEAD_DIM), ((0, 0), (0, 16 - n), (0, 0)))

    outs = {k: [] for k in ("akp", "avp", "aikp", "aks", "avs", "aiks", "bkp", "bvp", "blfp",
                            "bks", "bvs", "blfs")}
    for i in range(DEPTH):
        j = i // 2
        if i % 2 == 0:
            ymain = _project(h_bf, w_in_a, j, 0, (QS + 2 * KA + QI) // MM_TN, MM_TN, tm_mm)
            ykw = _project(h_bf, w_a_kw, j, 0, 1, LANES, tm_mm)
            yqm = _project(h_bf, w_a_qm, j, 0, 1, QM, tm_mm)
            k_all, v_all = ymain[:, QS:QS + KA], ymain[:, QS + KA:QS + 2 * KA]
            outs["akp"].append(k_all[:NP].reshape(B, S, KV_A, HEAD_DIM))
            outs["avp"].append(v_all[:NP].reshape(B, S, KV_A, HEAD_DIM))
            outs["aikp"].append(ykw[:NP, :IDX_DIM].reshape(B, S, IDX_DIM))
            outs["aks"].append(k_all[NP:NP + BD].reshape(BD, 1, KV_A, HEAD_DIM))
            outs["avs"].append(v_all[NP:NP + BD].reshape(BD, 1, KV_A, HEAD_DIM))
            outs["aiks"].append(ykw[NP:NP + BD, :IDX_DIM].reshape(BD, 1, IDX_DIM))
            attn = _prompt_attention_a(cfg, i, ymain, ykw, yqm, memkv)
            ys_, kws, qms = ymain[NP:NP + BD], ykw[NP:NP + BD], yqm[NP:NP + BD]
            qi_s = ys_[:, QS + 2 * KA:].reshape(BD, IDX_HEADS, IDX_DIM)
            wi_s = kws[:, IDX_DIM:IDX_DIM + IDX_HEADS].reshape(BD, IDX_HEADS, 1)
            ki_new = kws[:, :IDX_DIM].reshape(BD, 1, IDX_DIM)
            scores = _sample_idx_scores(cfg, j, page_table, qi_s, wi_s, ki_new, cache_a_ik)
            mask = _sample_select(cfg, scores.reshape(BD, -1)).reshape(BD, 1, -1)
            kn = ys_[:, QS:QS + KA].reshape(BD, KV_A, HEAD_DIM)
            vn = ys_[:, QS + KA:QS + 2 * KA].reshape(BD, KV_A, HEAD_DIM)
            rep = N_SELF // KV_A
            kn16 = jnp.pad(jnp.repeat(kn, rep, axis=1), ((0, 0), (0, 4), (0, 0)))
            vn16 = jnp.pad(jnp.repeat(vn, rep, axis=1), ((0, 0), (0, 4), (0, 0)))
            o_s, om_s = _sample_attention(
                cfg, j, i, page_table, heads16(ys_[:, :QS], N_SELF), kn16, vn16, mask,
                jnp.zeros((BD, 16, LANES), F32), slopes16, jnp.pad(qms.reshape(BD, N_MEMH, HEAD_DIM),
                                                                  ((0, 0), (0, 8 - N_MEMH), (0, 0))),
                pool_ak, pool_av, cmk, cmv, kvh=KV_A, fox=False)
        else:
            ymain = _project(h_bf, w_in_b, j, 0, (QS + 2 * KB) // MM_TN, MM_TN, tm_mm)
            lf = _project(h_bf, w_b_f, j, 0, 1, LANES, tm_mm, bias=b_f_pad)
            yqm = _project(h_bf, w_b_qm, j, 0, 1, QM, tm_mm)
            k_all, v_all = ymain[:, QS:QS + KB], ymain[:, QS + KB:QS + 2 * KB]
            outs["bkp"].append(k_all[:NP].reshape(B, S, KV_B, HEAD_DIM))
            outs["bvp"].append(v_all[:NP].reshape(B, S, KV_B, HEAD_DIM))
            outs["blfp"].append(lf[:NP, :N_SELF].reshape(B, S, N_SELF))
            outs["bks"].append(k_all[NP:NP + BD].reshape(BD, 1, KV_B, HEAD_DIM))
            outs["bvs"].append(v_all[NP:NP + BD].reshape(BD, 1, KV_B, HEAD_DIM))
            outs["blfs"].append(lf[NP:NP + BD, :N_SELF].reshape(BD, 1, N_SELF))
            c_tok, c_t = _prompt_cumsum(cfg, lf)
            attn = _prompt_attention_b(cfg, i, ymain, c_tok, c_t, yqm, memkv)
            ys_, qms = ymain[NP:NP + BD], yqm[NP:NP + BD]
            lf_new = jnp.broadcast_to(_pad_last(lf[NP:NP + BD, :N_SELF], 16)[:, :, None], (BD, 16, LANES))
            c_all, cq = _sample_cumsum(cfg, j, page_table, lf_new, pool_lft)
            kn = ys_[:, QS:QS + KB].reshape(BD, KV_B, HEAD_DIM)
            vn = ys_[:, QS + KB:QS + 2 * KB].reshape(BD, KV_B, HEAD_DIM)
            rep = N_SELF // KV_B
            kn16 = jnp.pad(jnp.repeat(kn, rep, axis=1), ((0, 0), (0, 4), (0, 0)))
            vn16 = jnp.pad(jnp.repeat(vn, rep, axis=1), ((0, 0), (0, 4), (0, 0)))
            o_s, om_s = _sample_attention(
                cfg, j, i, page_table, heads16(ys_[:, :QS], N_SELF), kn16, vn16, c_all, cq, slopes16,
                jnp.pad(qms.reshape(BD, N_MEMH, HEAD_DIM), ((0, 0), (0, 8 - N_MEMH), (0, 0))),
                pool_bk, pool_bv, cmk, cmv, kvh=KV_B, fox=True)
        attn_s = jnp.concatenate([o_s[:, :N_SELF].reshape(BD, QS), om_s[:, :N_MEMH].reshape(BD, QM)], -1)
        attn = attn.at[NP:].set(jnp.pad(attn_s, ((0, PAD_ROWS - BD), (0, 0))).astype(BF16))
        h, h_bf = _moe_layer(cfg, i, attn, h, w_o_bf, g1, b1, g2, b2, wr, br, w_gate, w_up, w_down)

    mk = memkv[:, :, :QM].reshape(DEPTH, B, N_MEM, N_MEMH, HEAD_DIM)
    mv = memkv[:, :, QM:].reshape(DEPTH, B, N_MEM, N_MEMH, HEAD_DIM)
    st = lambda k: jnp.stack(outs[k])
    return (h[:NP].reshape(B, S, D), h[NP:NP + BD].reshape(BD, 1, D),
            st("akp"), st("avp"), st("aikp"), st("bkp"), st("bvp"), st("blfp"), mk, mv,
            st("aks"), st("avs"), st("aiks"), st("bks"), st("bvs"), st("blfs"))


def kernel(x_prompt, x_sample, mem_prompt, cache_a_k, cache_a_v, cache_a_ik, cache_b_k, cache_b_v,
           cache_b_lf, cache_mem_k, cache_mem_v, page_table, w_in_a, w_in_b, b_f, w_mem_kv, w_o,
           ln1_g, ln1_b, ln2_g, ln2_b, w_r1, b_r1, w_r2, b_r2, w_gate, w_up, w_down):
    B, S, D = x_prompt.shape
    BD = x_sample.shape[0]
    cfg = Cfg(D=D, B=B, S=S, BD=BD, PAST=page_table.shape[1] * PAGE, DEPTH=w_o.shape[0], PP=8)
    return _forward(cfg, x_prompt, x_sample, mem_prompt, cache_a_k, cache_a_v, cache_a_ik, cache_b_k,
                    cache_b_v, cache_b_lf, cache_mem_k, cache_mem_v, page_table, w_in_a, w_in_b, b_f,
                    w_mem_kv, w_o, ln1_g, ln1_b, ln2_g, ln2_b, w_r1, b_r1, w_r2, b_r2, w_gate, w_up,
                    w_down)
```
